```python
import math
import jax, jax.numpy as jnp
from jax import lax
import numpy as np

D_MODEL = 4096
BATCH = 4
SEQ = 4096
DEPTH = 4

GRID_W = 64
CTX_LEN = 256
N_MIXERS = 2
RET_HEADS = 16
RET_HEAD_DIM = D_MODEL // RET_HEADS
RET_CHUNK = 128
ROPE_THETA = 10000.0
POOL_WINDOWS = (2, 4, 8, 16)
POOL_GROUPS = len(POOL_WINDOWS)
POOL_GROUP_DIM = D_MODEL // POOL_GROUPS
D_FF = 11 * D_MODEL // 8
ADA_RANK = D_MODEL // 8
N_MOD = 9
EPS = 1e-6

kernel_name = "hybrid_retention_pool_macaron_dit"


def rms_norm(t, gain):
    tf = t.astype(jnp.float32)
    out = tf * lax.rsqrt(jnp.mean(tf * tf, axis=-1, keepdims=True) + EPS)
    return out.astype(t.dtype) * gain


def ada_modulation(cvec, w_down, w_up, b):
    m = (jax.nn.silu(cvec) @ w_down) @ w_up + b
    return m.reshape(cvec.shape[0], N_MOD, 1, D_MODEL)


def pre_mod(t, mods, k, gain):
    return rms_norm(t, gain) * (1.0 + mods[:, 3 * k + 1]) + mods[:, 3 * k]


def post_add(t, y, mods, k, gain, weight):
    return t + weight * mods[:, 3 * k + 2] * rms_norm(y, gain)


def swiglu(h, w_gate, w_up, w_down):
    return (jax.nn.silu(h @ w_gate) * (h @ w_up)) @ w_down


def axial_rope_tables(rows, cols):
    n_freq = RET_HEAD_DIM // 4
    inv = ROPE_THETA ** (-jnp.arange(n_freq, dtype=jnp.float32) / n_freq)
    ang = jnp.concatenate([rows[:, None].astype(jnp.float32) * inv,
                           cols[:, None].astype(jnp.float32) * inv], axis=-1)
    return jnp.cos(ang), jnp.sin(ang)


def apply_rope(t, cos, sin):
    t2 = t.reshape(*t.shape[:-1], RET_HEAD_DIM // 2, 2)
    a, b = t2[..., 0], t2[..., 1]
    return jnp.stack([a * cos - b * sin, a * sin + b * cos], axis=-1).reshape(t.shape)


def split_heads(t):
    b, l, _ = t.shape
    return t.reshape(b, l, RET_HEADS, RET_HEAD_DIM).transpose(0, 2, 1, 3).astype(jnp.float32)


def head_norm(o):
    mu = jnp.mean(o, axis=-1, keepdims=True)
    d = o - mu
    o = d * lax.rsqrt(jnp.mean(d * d, axis=-1, keepdims=True) + EPS)
    b, h, l, dh = o.shape
    return o.transpose(0, 2, 1, 3).reshape(b, l, h * dh)


def chunk_retention(q, k, v, log_g, s0):
    b, h, l, dh = q.shape
    nc = l // RET_CHUNK
    idx = jnp.arange(RET_CHUNK, dtype=jnp.float32)
    lg = log_g[:, None]
    rel = idx[:, None] - idx[None, :]
    inner = jnp.where(rel >= 0, jnp.exp(lg[..., None] * jnp.maximum(rel, 0.0)), 0.0)
    q_decay = jnp.exp(lg * (idx + 1.0))[:, :, None]
    k_decay = jnp.exp(lg * (RET_CHUNK - 1.0 - idx))[:, :, None]
    chunk_decay = jnp.exp(log_g * RET_CHUNK)[:, None, None]

    def to_chunks(t):
        return t.reshape(b, h, nc, RET_CHUNK, t.shape[-1]).transpose(2, 0, 1, 3, 4)

    def step(s, qkv):
        qc, kc, vc = qkv
        scores = jnp.einsum('bhid,bhjd->bhij', qc, kc) * inner
        o = (jnp.einsum('bhij,bhjv->bhiv', scores, vc)
             + jnp.einsum('bhid,bhdv->bhiv', qc * q_decay, s))
        s = chunk_decay * s + jnp.einsum('bhjd,bhjv->bhdv', kc * k_decay, vc)
        return s, o

    s_fin, o = lax.scan(step, s0, (to_chunks(q), to_chunks(k), to_chunks(v)))
    return o.transpose(1, 2, 0, 3, 4).reshape(b, h, l, dh), s_fin


def retention_final_state(k, v, log_g):
    l = k.shape[2]
    w = jnp.exp(log_g[:, None] * (l - 1.0 - jnp.arange(l, dtype=jnp.float32)))
    return jnp.einsum('bhld,hl,bhlv->bhdv', k, w, v)


def retention_mixer(hx, hc, w_in, w_out, decay_logit, rope_cos, rope_sin, ctx_out):
    d = D_MODEL
    log_g = jax.nn.log_sigmoid(decay_logit.astype(jnp.float32))
    scale = RET_HEAD_DIM ** -0.5
    qx, kx, vx, gfx, gbx = jnp.split(hx @ w_in, 5, axis=-1)
    n_cols = 5 if ctx_out else 3
    proj_c = hc @ w_in[:, :n_cols * d]
    qc, kc, vc = split_heads(proj_c[..., :d]) * scale, split_heads(proj_c[..., d:2 * d]), split_heads(proj_c[..., 2 * d:3 * d])
    qx = apply_rope(split_heads(qx), rope_cos, rope_sin) * scale
    kx = apply_rope(split_heads(kx), rope_cos, rope_sin)
    vx = split_heads(vx)
    flip = lambda t: jnp.flip(t, axis=2)
    s_zero = jnp.zeros((hc.shape[0], RET_HEADS, RET_HEAD_DIM, RET_HEAD_DIM), jnp.float32)
    if ctx_out:
        oc_f, s_f = chunk_retention(qc, kc, vc, log_g[0], s_zero)
        oc_b, s_b = chunk_retention(flip(qc), flip(kc), flip(vc), log_g[1], s_zero)
        gfc, gbc = proj_c[..., 3 * d:4 * d], proj_c[..., 4 * d:]
        yc = (head_norm(oc_f) * jax.nn.silu(gfc) + head_norm(flip(oc_b)) * jax.nn.silu(gbc)).astype(hc.dtype) @ w_out
    else:
        s_f = retention_final_state(kc, vc, log_g[0])
        s_b = retention_final_state(flip(kc), flip(vc), log_g[1])
        yc = None
    ox_f, _ = chunk_retention(qx, kx, vx, log_g[0], s_f)
    ox_b, _ = chunk_retention(flip(qx), flip(kx), flip(vx), log_g[1], s_b)
    yx = (head_norm(ox_f) * jax.nn.silu(gfx) + head_norm(flip(ox_b)) * jax.nn.silu(gbx)).astype(hx.dtype) @ w_out
    return yx, yc


def centred_box_sum(t, axis, w):
    n = t.shape[axis]
    pad = [(0, 0)] * t.ndim
    pad[axis] = (1, 0)
    cs = jnp.pad(jnp.cumsum(t, axis=axis), pad)
    i = jnp.arange(n)
    lo = jnp.clip(i - w // 2, 0, n)
    hi = jnp.clip(i + w // 2, 0, n)
    return jnp.take(cs, hi, axis=axis) - jnp.take(cs, lo, axis=axis), (hi - lo)


def multi_scale_pool(h, w_pool, scale, grid_rows):
    b, l, d = h.shape
    hf = h.astype(jnp.float32)
    if grid_rows is None:
        hf = hf.reshape(b, l, POOL_GROUPS, POOL_GROUP_DIM)
    else:
        hf = hf.reshape(b, grid_rows, GRID_W, POOL_GROUPS, POOL_GROUP_DIM)
    outs = []
    for g, w in enumerate(POOL_WINDOWS):
        t = hf[..., g, :]
        if grid_rows is None:
            s, cnt = centred_box_sum(t, 1, w)
            mean = s / cnt.astype(jnp.float32)[None, :, None]
        else:
            s, cr = centred_box_sum(t, 1, w)
            s, cc = centred_box_sum(s, 2, w)
            cnt = (cr[:, None] * cc[None, :]).astype(jnp.float32)
            mean = s / cnt[None, :, :, None]
        outs.append(mean - t)
    p = jnp.stack(outs, axis=-2).reshape(b, l, POOL_GROUPS, POOL_GROUP_DIM).astype(h.dtype)
    y = jnp.einsum('blgc,gce->blge', p, w_pool).reshape(b, l, d)
    return y * scale


def setup_inputs(seed: int = 0) -> dict:
    key = jax.random.key(seed)
    ks = jax.random.split(key, 16)
    d = D_MODEL
    n_ret = len(range(0, DEPTH, N_MIXERS))
    n_pool = len(range(1, DEPTH, N_MIXERS))
    nrm = jax.random.normal
    hidx = jnp.arange(RET_HEADS, dtype=jnp.float32)
    base_logit = jnp.log1p(-jnp.exp2(-5.0 - hidx)) + (5.0 + hidx) * math.log(2.0)
    return {
        "x": nrm(ks[0], (BATCH, SEQ, d), jnp.float32),
        "c": nrm(ks[1], (BATCH, d), jnp.float32),
        "ctx": nrm(ks[2], (BATCH, CTX_LEN, d), jnp.float32),
        "c_ctx": nrm(ks[3], (d,), jnp.float32),
        "w_ada_down": nrm(ks[4], (DEPTH, d, ADA_RANK), jnp.float32) * d ** -0.5,
        "w_ada_up": nrm(ks[5], (DEPTH, ADA_RANK, N_MOD * d), jnp.float32) * (0.8 * ADA_RANK ** -0.5),
        "b_ada": 0.02 * nrm(ks[6], (DEPTH, N_MOD * d), jnp.float32),
        "norm_gain": 1.0 + 0.05 * nrm(ks[7], (DEPTH, 6, d), jnp.float32),
        "w_ffn_gate": nrm(ks[8], (DEPTH, 2, d, D_FF), jnp.float32) * d ** -0.5,
        "w_ffn_up": nrm(ks[9], (DEPTH, 2, d, D_FF), jnp.float32) * d ** -0.5,
        "w_ffn_down": nrm(ks[10], (DEPTH, 2, D_FF, d), jnp.float32) * D_FF ** -0.5,
        "w_ret_in": nrm(ks[11], (n_ret, d, 5 * d), jnp.float32) * d ** -0.5,
        "w_ret_out": nrm(ks[12], (n_ret, d, d), jnp.float32) * d ** -0.5,
        "ret_decay_logit": base_logit[None, None, :] + 0.1 * nrm(ks[13], (n_ret, 2, RET_HEADS), jnp.float32),
        "w_pool": nrm(ks[14], (n_pool, POOL_GROUPS, POOL_GROUP_DIM, POOL_GROUP_DIM), jnp.float32) * POOL_GROUP_DIM ** -0.5,
        "pool_scale": 1.0 + 0.1 * nrm(ks[15], (n_pool, d), jnp.float32),
    }


def reference(x, c, ctx, c_ctx, w_ada_down, w_ada_up, b_ada, norm_gain, w_ffn_gate, w_ffn_up,
              w_ffn_down, w_ret_in, w_ret_out, ret_decay_logit, w_pool, pool_scale):
    seq = x.shape[1]
    rows = seq // GRID_W
    pos = jnp.arange(seq)
    rope_cos, rope_sin = axial_rope_tables(pos // GRID_W, pos % GRID_W)
    last_ret = ((DEPTH - 1) // N_MIXERS) * N_MIXERS
    for i in range(DEPTH):
        kind = i % N_MIXERS
        j = i // N_MIXERS
        ctx_in = i <= last_ret
        ctx_out = i < last_ret
        g = norm_gain[i]
        mx = ada_modulation(c, w_ada_down[i], w_ada_up[i], b_ada[i])
        x = post_add(x, swiglu(pre_mod(x, mx, 0, g[0]), w_ffn_gate[i, 0], w_ffn_up[i, 0], w_ffn_down[i, 0]),
                     mx, 0, g[1], 0.5)
        if ctx_in:
            mc = ada_modulation(c_ctx[None, :], w_ada_down[i], w_ada_up[i], b_ada[i])
            ctx = post_add(ctx, swiglu(pre_mod(ctx, mc, 0, g[0]), w_ffn_gate[i, 0], w_ffn_up[i, 0], w_ffn_down[i, 0]),
                           mc, 0, g[1], 0.5)
        hx = pre_mod(x, mx, 1, g[2])
        if kind == 0:
            hc = pre_mod(ctx, mc, 1, g[2])
            yx, yc = retention_mixer(hx, hc, w_ret_in[j], w_ret_out[j], ret_decay_logit[j],
                                     rope_cos, rope_sin, ctx_out)
        else:
            yx = multi_scale_pool(hx, w_pool[j], pool_scale[j], rows)
            yc = multi_scale_pool(pre_mod(ctx, mc, 1, g[2]), w_pool[j], pool_scale[j], None) if ctx_out else None
        x = post_add(x, yx, mx, 1, g[3], 1.0)
        x = post_add(x, swiglu(pre_mod(x, mx, 2, g[4]), w_ffn_gate[i, 1], w_ffn_up[i, 1], w_ffn_down[i, 1]),
                     mx, 2, g[5], 0.5)
        if ctx_out:
            ctx = post_add(ctx, yc, mc, 1, g[3], 1.0)
            ctx = post_add(ctx, swiglu(pre_mod(ctx, mc, 2, g[4]), w_ffn_gate[i, 1], w_ffn_up[i, 1], w_ffn_down[i, 1]),
                           mc, 2, g[5], 0.5)
    return x
```

```python
import functools
import math

import numpy as np
import jax
import jax.numpy as jnp
from jax import lax
from jax.experimental import pallas as pl
from jax.experimental.pallas import tpu as pltpu

GRID_W = 64
RET_HEAD_DIM = 256
RET_CHUNK = 128
ROPE_THETA = 10000.0
POOL_WINDOWS = (2, 4, 8, 16)
N_MIXERS = 2
N_MOD = 9
EPS = 1e-6

LANES = 128
MXU_DIM = 256
ROW_TILE = 256
MM_ROW_TILE = 1024
MM_COL_TILE = 512
POOL_SLAB = 256
POOL_PAD_ROWS = 8
MIB = 1024 * 1024

f32 = jnp.float32
bf16 = jnp.bfloat16


def _pick(n, pref, align):
    t = min(pref, n)
    t -= t % align
    while t >= align:
        if n % t == 0:
            return t
        t -= align
    raise ValueError(f"no tile for {n} (pref {pref}, align {align})")


def _params(vmem_mib, n_grid):
    return pltpu.CompilerParams(
        dimension_semantics=("arbitrary",) * n_grid,
        vmem_limit_bytes=int(vmem_mib * MIB))


def _dot(a, b):
    return jnp.dot(a, b, preferred_element_type=f32)


def _ada_kernel(c_ref, wd_ref, wu_ref, b_ref, o_ref, t_ref):
    @pl.when(pl.program_id(1) == 0)
    def _():
        cv = c_ref[...]
        s = (cv * jax.nn.sigmoid(cv)).astype(bf16)
        t_ref[...] = _dot(s, wd_ref[0].astype(bf16))

    o_ref[0, 0] = _dot(t_ref[...].astype(bf16), wu_ref[0].astype(bf16)) + b_ref[0, 0]


def _ada_modulation(cvecs, w_down, w_up, b):
    depth, d, r = w_down.shape
    m = cvecs.shape[0]
    return pl.pallas_call(
        _ada_kernel,
        grid=(depth, N_MOD),
        in_specs=[
            pl.BlockSpec((m, d), lambda l, j: (0, 0)),
            pl.BlockSpec((1, d, r), lambda l, j: (l, 0, 0)),
            pl.BlockSpec((1, r, d), lambda l, j: (l, 0, j)),
            pl.BlockSpec((1, 1, 1, d), lambda l, j: (l, j, 0, 0)),
        ],
        out_specs=pl.BlockSpec((1, 1, m, d), lambda l, j: (l, j, 0, 0)),
        out_shape=jax.ShapeDtypeStruct((depth, N_MOD, m, d), f32),
        scratch_shapes=[pltpu.VMEM((m, r), f32)],
        compiler_params=_params(48, 2),
        name="ada_modulation",
    )(cvecs, w_down, w_up, b.reshape(depth, N_MOD, 1, d))


def _rowwise_kernel(*refs, post, pre, inv_d):
    it = iter(refs)
    x_ref = next(it)
    if post is not None:
        y_ref, mpost_ref, gpost_ref = next(it), next(it), next(it)
    if pre is not None:
        mpre_ref, gpre_ref = next(it), next(it)
    if post is not None:
        xo_ref = next(it)
    if pre is not None:
        h_ref = next(it)

    x = x_ref[...]
    if post is not None:
        k, weight = post
        y = y_ref[...]
        ms = jnp.sum(y * y, axis=-1, keepdims=True) * inv_d
        coef = (weight * mpost_ref[0, 3 * k + 2:3 * k + 3, :]) * gpost_ref[2 * k + 1:2 * k + 2, :]
        x = x + (y * lax.rsqrt(ms + EPS)) * coef
        xo_ref[...] = x
    if pre is not None:
        k = pre
        ms = jnp.sum(x * x, axis=-1, keepdims=True) * inv_d
        amp = gpre_ref[2 * k:2 * k + 1, :] * (1.0 + mpre_ref[0, 3 * k + 1:3 * k + 2, :])
        h = (x * lax.rsqrt(ms + EPS)) * amp + mpre_ref[0, 3 * k:3 * k + 1, :]
        h_ref[...] = h.astype(h_ref.dtype)


def _rowwise(x, y, post, pre, *, n_batch, x_tiles, c_tiles_in, c_tiles_out, h_dtype):
    d = x.shape[1]
    tm = ROW_TILE
    per_in = x_tiles + c_tiles_in
    per_out = x_tiles + c_tiles_out
    n_tiles = n_batch * per_out

    def in_tile(t):
        return (t // per_out) * per_in + t % per_out

    def mod_row(t):
        return jnp.where(t % per_out < x_tiles, t // per_out, n_batch)

    row_spec = pl.BlockSpec((tm, d), lambda t: (in_tile(t), 0))
    out_spec = pl.BlockSpec((tm, d), lambda t: (t, 0))
    mod_spec = pl.BlockSpec((1, N_MOD, d), lambda t: (mod_row(t), 0, 0))
    gain_spec = pl.BlockSpec((6, d), lambda t: (0, 0))

    args, in_specs, out_specs, out_shape = [x], [row_spec], [], []
    if post is not None:
        args += [y, post[0], post[1]]
        in_specs += [row_spec, mod_spec, gain_spec]
        out_specs.append(out_spec)
        out_shape.append(jax.ShapeDtypeStruct((n_tiles * tm, d), f32))
    if pre is not None:
        args += [pre[0], pre[1]]
        in_specs += [mod_spec, gain_spec]
        out_specs.append(out_spec)
        out_shape.append(jax.ShapeDtypeStruct((n_tiles * tm, d), h_dtype))

    kern = functools.partial(
        _rowwise_kernel,
        post=None if post is None else (post[2], post[3]),
        pre=None if pre is None else pre[2],
        inv_d=1.0 / d)
    return pl.pallas_call(
        kern,
        grid=(n_tiles,),
        in_specs=in_specs,
        out_specs=out_specs,
        out_shape=out_shape,
        compiler_params=_params(48, 1),
        name="rowwise",
    )(*args)


def _mm_kernel(a_ref, w_ref, o_ref):
    o_ref[...] = _dot(a_ref[...], w_ref[...]).astype(o_ref.dtype)


def _matmul(a, w, out_dtype):
    m, k = a.shape
    n = w.shape[1]
    tm = _pick(m, MM_ROW_TILE, 8)
    tn = _pick(n, MM_COL_TILE, LANES)
    return pl.pallas_call(
        _mm_kernel,
        grid=(m // tm, n // tn),
        in_specs=[pl.BlockSpec((tm, k), lambda i, j: (i, 0)),
                  pl.BlockSpec((k, tn), lambda i, j: (0, j))],
        out_specs=pl.BlockSpec((tm, tn), lambda i, j: (i, j)),
        out_shape=jax.ShapeDtypeStruct((m, n), out_dtype),
        compiler_params=_params(56, 2),
        name="matmul",
    )(a, w)


def _swiglu_kernel(h_ref, wg_ref, wu_ref, o_ref):
    h = h_ref[...]
    g = _dot(h, wg_ref[...])
    u = _dot(h, wu_ref[...])
    o_ref[...] = (g * jax.nn.sigmoid(g) * u).astype(o_ref.dtype)


def _swiglu_up(h, w_gate, w_up):
    m, k = h.shape
    n = w_gate.shape[1]
    tm = _pick(m, MM_ROW_TILE, 8)
    tn = _pick(n, MM_COL_TILE, LANES)
    return pl.pallas_call(
        _swiglu_kernel,
        grid=(m // tm, n // tn),
        in_specs=[pl.BlockSpec((tm, k), lambda i, j: (i, 0)),
                  pl.BlockSpec((k, tn), lambda i, j: (0, j)),
                  pl.BlockSpec((k, tn), lambda i, j: (0, j))],
        out_specs=pl.BlockSpec((tm, tn), lambda i, j: (i, j)),
        out_shape=jax.ShapeDtypeStruct((m, n), bf16),
        compiler_params=_params(56, 2),
        name="swiglu_up",
    )(h, w_gate, w_up)


def _ffn(h, w_gate, w_up, w_down):
    return _matmul(_swiglu_up(h, w_gate, w_up), w_down, f32)


def _inproj_kernel(h_ref, w_ref, cos_ref, sin_ref, o_ref, *, d_tiles, scale):
    j = pl.program_id(1)
    acc = _dot(h_ref[...], w_ref[...])
    half = RET_HEAD_DIM // 2

    def rope(mult):
        cos, sin = cos_ref[...], sin_ref[...]
        for hh in range(acc.shape[1] // RET_HEAD_DIM):
            c0 = hh * RET_HEAD_DIM
            a = acc[:, c0:c0 + half]
            b = acc[:, c0 + half:c0 + RET_HEAD_DIM]
            o_ref[:, c0:c0 + half] = ((a * cos - b * sin) * mult).astype(o_ref.dtype)
            o_ref[:, c0 + half:c0 + RET_HEAD_DIM] = ((a * sin + b * cos) * mult).astype(o_ref.dtype)

    @pl.when(j < d_tiles)
    def _():
        rope(scale)

    @pl.when(jnp.logical_and(j >= d_tiles, j < 2 * d_tiles))
    def _():
        rope(1.0)

    @pl.when(jnp.logical_and(j >= 2 * d_tiles, j < 3 * d_tiles))
    def _():
        o_ref[...] = acc.astype(o_ref.dtype)

    @pl.when(j >= 3 * d_tiles)
    def _():
        o_ref[...] = (acc * jax.nn.sigmoid(acc)).astype(o_ref.dtype)


def _ret_inproj(h, w_in, cos, sin):
    m, d = h.shape
    n = w_in.shape[1]
    tm = _pick(m, MM_ROW_TILE, 8)
    tn = _pick(d, MM_COL_TILE, RET_HEAD_DIM)
    kern = functools.partial(_inproj_kernel, d_tiles=d // tn, scale=RET_HEAD_DIM ** -0.5)
    return pl.pallas_call(
        kern,
        grid=(m // tm, n // tn),
        in_specs=[pl.BlockSpec((tm, d), lambda i, j: (i, 0)),
                  pl.BlockSpec((d, tn), lambda i, j: (0, j)),
                  pl.BlockSpec((tm, RET_HEAD_DIM // 2), lambda i, j: (i, 0)),
                  pl.BlockSpec((tm, RET_HEAD_DIM // 2), lambda i, j: (i, 0))],
        out_specs=pl.BlockSpec((tm, tn), lambda i, j: (i, j)),
        out_shape=jax.ShapeDtypeStruct((m, n), bf16),
        compiler_params=_params(56, 2),
        name="ret_inproj",
    )(h, w_in, cos, sin)


def _ret_kernel(lg_ref, q_ref, k_ref, v_ref, gf_ref, gb_ref, o_ref, s_ref, acc_ref, tab_ref,
                *, n_x, n_c, ctx_out):
    C, DH = RET_CHUNK, RET_HEAD_DIM
    ri = lax.broadcasted_iota(jnp.int32, (C, C), 0).astype(f32)
    ci = lax.broadcasted_iota(jnp.int32, (C, C), 1).astype(f32)
    rw = lax.broadcasted_iota(jnp.int32, (C, DH), 0).astype(f32)
    for d in range(2):
        z = lg_ref[d, 0]
        lg = jnp.minimum(z, 0.0) - jnp.log1p(jnp.exp(-jnp.abs(z)))
        rel = (ri - ci) if d == 0 else (ci - ri)
        tab_ref[d, 0, :, :C] = jnp.where(rel >= 0, jnp.exp(lg[:, :C] * jnp.maximum(rel, 0.0)), 0.0)
        tab_ref[d, 1] = jnp.exp(lg * (rw + 1.0)) if d == 0 else jnp.exp(lg * (C - rw))
        tab_ref[d, 2] = jnp.exp(lg * (C - 1.0 - rw)) if d == 0 else jnp.exp(lg * rw)
        tab_ref[d, 3] = jnp.exp(lg * (0.0 * rw + C))
    s_ref[...] = jnp.zeros_like(s_ref)
    gate_refs = (gf_ref, gb_ref)

    def one_dir(d, c, first, want_o):
        r0 = c * C
        if not isinstance(r0, int):
            r0 = pl.multiple_of(r0, C)
        rows = pl.ds(r0, C)
        q = q_ref[0, rows, :]
        k = k_ref[0, rows, :]
        v = v_ref[0, rows, :]
        s_old = s_ref[d]
        kd = (k.astype(f32) * tab_ref[d, 2]).astype(bf16)
        s_ref[d] = tab_ref[d, 3, 0:1, :] * s_old + lax.dot_general(
            kd, v, (((0,), (0,)), ((), ())), preferred_element_type=f32)
        if not want_o:
            return
        scores = lax.dot_general(q, k, (((1,), (1,)), ((), ())), preferred_element_type=f32)
        pm = (scores * tab_ref[d, 0, :, :C]).astype(bf16)
        o = _dot(pm, v) + tab_ref[d, 1] * _dot(q, s_old.astype(bf16))
        mu = jnp.mean(o, axis=-1, keepdims=True)
        dl = o - mu
        var = jnp.mean(dl * dl, axis=-1, keepdims=True)
        res = dl * lax.rsqrt(var + EPS) * gate_refs[d][0, rows, :].astype(f32)
        if first:
            acc_ref[rows, :] = res
        else:
            o_ref[0, rows, :] = (acc_ref[rows, :] + res).astype(o_ref.dtype)

    def phase(base, n, want_o):
        assert n % 2 == 0
        for first, lo in ((True, 0), (False, n // 2)):
            if n // 2 <= 2:
                for t in range(lo, lo + n // 2):
                    one_dir(0, base + t, first, want_o)
                    one_dir(1, base + n - 1 - t, first, want_o)
            else:
                def body(t, carry, first=first):
                    one_dir(0, base + t, first, want_o)
                    one_dir(1, base + n - 1 - t, first, want_o)
                    return carry
                lax.fori_loop(lo, lo + n // 2, body, 0)

    phase(n_x, n_c, ctx_out)
    if not ctx_out:
        o_ref[0, n_x * C:(n_x + n_c) * C, :] = jnp.zeros((n_c * C, DH), o_ref.dtype)
    phase(0, n_x, True)


def _retention_core(p, decay_logit, *, n_batch, n_heads, n_x, n_c, ctx_out):
    rows = (n_x + n_c) * RET_CHUNK
    dh = RET_HEAD_DIM
    lg = jnp.broadcast_to(decay_logit.astype(f32)[:, :, None, None], (2, n_heads, 1, dh))

    def col(g):
        return pl.BlockSpec((1, rows, dh), lambda b, h: (b, 0, g * n_heads + h))

    kern = functools.partial(_ret_kernel, n_x=n_x, n_c=n_c, ctx_out=ctx_out)
    return pl.pallas_call(
        kern,
        grid=(n_batch, n_heads),
        in_specs=[pl.BlockSpec((2, 1, 1, dh), lambda b, h: (0, h, 0, 0)),
                  col(0), col(1), col(2), col(3), col(4)],
        out_specs=pl.BlockSpec((1, rows, dh), lambda b, h: (b, 0, h)),
        out_shape=jax.ShapeDtypeStruct((n_batch, rows, n_heads * dh), bf16),
        scratch_shapes=[pltpu.VMEM((2, dh, dh), f32),
                        pltpu.VMEM((rows, dh), f32),
                        pltpu.VMEM((2, 4, RET_CHUNK, dh), f32)],
        compiler_params=_params(56, 2),
        name="retention_core",
    )(lg, p, p, p, p, p)


def _rope_tables(n_batch, seq, ctx_len):
    n_freq = RET_HEAD_DIM // 4
    inv = ROPE_THETA ** (-jnp.arange(n_freq, dtype=f32) / n_freq)
    pos = jnp.arange(seq)
    ang = jnp.concatenate([(pos // GRID_W)[:, None].astype(f32) * inv,
                           (pos % GRID_W)[:, None].astype(f32) * inv], axis=-1)
    cos = jnp.concatenate([jnp.cos(ang), jnp.ones((ctx_len, 2 * n_freq), f32)], axis=0)
    sin = jnp.concatenate([jnp.sin(ang), jnp.zeros((ctx_len, 2 * n_freq), f32)], axis=0)
    return jnp.tile(cos, (n_batch, 1)), jnp.tile(sin, (n_batch, 1))


def _prep_ret_in(w_in, d):
    n_heads = d // RET_HEAD_DIM

    def deinterleave(w):
        w = w.reshape(d, n_heads, RET_HEAD_DIM // 2, 2)
        return jnp.swapaxes(w, 2, 3).reshape(d, d)

    wq, wk, rest = w_in[:, :d], w_in[:, d:2 * d], w_in[:, 2 * d:]
    return jnp.concatenate([deinterleave(wq), deinterleave(wk), rest], axis=1).astype(bf16)


def _pool_kernel(h_ref, band_ref, o_ref, pad_ref, *, n_x, n_c, grid_rows):
    g = pl.program_id(1)
    top = POOL_PAD_ROWS * GRID_W
    ct = h_ref.shape[2]

    def band_sum(band, t):
        hi = t.astype(bf16)
        lo = (t - hi.astype(f32)).astype(bf16)
        return _dot(band, hi) + _dot(band, lo)

    def body(w):
        half = w // 2
        band2d = band_ref[0, 0]
        pad_ref[0:top, :] = jnp.zeros((top, ct), f32)
        pad_ref[top + n_x:2 * top + n_x, :] = jnp.zeros((top, ct), f32)

        def col_pass(s, carry):
            r0 = pl.multiple_of(s * POOL_SLAB, POOL_SLAB)
            pad_ref[pl.ds(top + r0, POOL_SLAB), :] = band_sum(band2d, h_ref[0, pl.ds(r0, POOL_SLAB), :])
            return carry

        lax.fori_loop(0, n_x // POOL_SLAB, col_pass, 0)

        def row_pass(s, carry):
            r0 = pl.multiple_of(s * POOL_SLAB, POOL_SLAB)
            acc = pad_ref[pl.ds(top + r0 - half * GRID_W, POOL_SLAB), :]
            for dd in range(-half + 1, half):
                acc = acc + pad_ref[pl.ds(top + r0 + dd * GRID_W, POOL_SLAB), :]
            tok = r0 + lax.broadcasted_iota(jnp.int32, (POOL_SLAB, ct), 0)
            r = tok // GRID_W
            c = tok % GRID_W
            cr = jnp.minimum(r + half, grid_rows) - jnp.maximum(r - half, 0)
            cc = jnp.minimum(c + half, GRID_W) - jnp.maximum(c - half, 0)
            cnt = (cr * cc).astype(f32)
            rows = pl.ds(r0, POOL_SLAB)
            o_ref[0, rows, :] = (acc / cnt - h_ref[0, rows, :]).astype(o_ref.dtype)
            return carry

        lax.fori_loop(0, n_x // POOL_SLAB, row_pass, 0)

        if n_c:
            t = h_ref[0, n_x:n_x + n_c, :]
            i = lax.broadcasted_iota(jnp.int32, (n_c, ct), 0)
            cnt = (jnp.minimum(i + half, n_c) - jnp.maximum(i - half, 0)).astype(f32)
            o_ref[0, n_x:n_x + n_c, :] = (band_sum(band_ref[0, 1], t) / cnt - t).astype(o_ref.dtype)

    for gi, w in enumerate(POOL_WINDOWS):
        pl.when(g == gi)(functools.partial(body, w))


def _pool_bands(n_c):
    n1 = n_c if n_c else POOL_SLAB
    assert n1 == POOL_SLAB
    i = np.arange(POOL_SLAB)
    bands = np.zeros((len(POOL_WINDOWS), 2, POOL_SLAB, POOL_SLAB), np.float32)
    for gi, w in enumerate(POOL_WINDOWS):
        half = w // 2
        c = i % GRID_W
        lo = (i - c) + np.clip(c - half, 0, GRID_W)
        hi = (i - c) + np.clip(c + half, 0, GRID_W)
        bands[gi, 0] = (i[None, :] >= lo[:, None]) & (i[None, :] < hi[:, None])
        lo1 = np.clip(i - half, 0, n1)
        hi1 = np.clip(i + half, 0, n1)
        bands[gi, 1] = (i[None, :] >= lo1[:, None]) & (i[None, :] < hi1[:, None])
    return jnp.asarray(bands, bf16)


def _pool_means(h, *, n_x, n_c):
    n_batch, rows, d = h.shape
    n_groups = len(POOL_WINDOWS)
    gd = d // n_groups
    ct = _pick(gd, 256, LANES)
    tpg = gd // ct
    grid_rows = n_x // GRID_W
    assert n_x % POOL_SLAB == 0 and POOL_SLAB % GRID_W == 0
    kern = functools.partial(_pool_kernel, n_x=n_x, n_c=n_c, grid_rows=grid_rows)
    blk = pl.BlockSpec((1, rows, ct), lambda b, g, j: (b, 0, g * tpg + j))
    return pl.pallas_call(
        kern,
        grid=(n_batch, n_groups, tpg),
        in_specs=[blk, pl.BlockSpec((1, 2, POOL_SLAB, POOL_SLAB), lambda b, g, j: (g, 0, 0, 0))],
        out_specs=blk,
        out_shape=jax.ShapeDtypeStruct((n_batch, rows, d), bf16),
        scratch_shapes=[pltpu.VMEM((n_x + 2 * POOL_PAD_ROWS * GRID_W, ct), f32)],
        compiler_params=_params(48, 3),
        name="pool_means",
    )(h, _pool_bands(n_c))


def _pool_mm_kernel(p_ref, w_ref, s_ref, o_ref):
    o_ref[...] = _dot(p_ref[...], w_ref[0]) * s_ref[...]


def _pool_project(p, w_pool, scale):
    m, d = p.shape
    n_groups, gd, _ = w_pool.shape
    tm = _pick(m, MM_ROW_TILE, 8)
    return pl.pallas_call(
        _pool_mm_kernel,
        grid=(m // tm, n_groups),
        in_specs=[pl.BlockSpec((tm, gd), lambda i, g: (i, g)),
                  pl.BlockSpec((1, gd, gd), lambda i, g: (g, 0, 0)),
                  pl.BlockSpec((1, gd), lambda i, g: (0, g))],
        out_specs=pl.BlockSpec((tm, gd), lambda i, g: (i, g)),
        out_shape=jax.ShapeDtypeStruct((m, d), f32),
        compiler_params=_params(48, 2),
        name="pool_project",
    )(p, w_pool, scale.reshape(1, d))


def kernel(x, c, ctx, c_ctx, w_ada_down, w_ada_up, b_ada, norm_gain, w_ffn_gate, w_ffn_up,
           w_ffn_down, w_ret_in, w_ret_out, ret_decay_logit, w_pool, pool_scale):
    n_batch, seq, d = x.shape
    ctx_len = ctx.shape[1]
    depth = w_ada_down.shape[0]
    n_heads = d // RET_HEAD_DIM
    assert seq % ROW_TILE == 0 and ctx_len % ROW_TILE == 0 and n_batch < 8
    x_tiles, c_tiles = seq // ROW_TILE, ctx_len // ROW_TILE
    last_ret = ((depth - 1) // N_MIXERS) * N_MIXERS

    wg, wu, wd = w_ffn_gate.astype(bf16), w_ffn_up.astype(bf16), w_ffn_down.astype(bf16)
    w_rout, w_pl = w_ret_out.astype(bf16), w_pool.astype(bf16)
    cos, sin = _rope_tables(n_batch, seq, ctx_len)

    cvecs = jnp.zeros((8, d), f32).at[:n_batch].set(c).at[n_batch].set(c_ctx)
    mods_all = _ada_modulation(cvecs, w_ada_down, w_ada_up, b_ada)
    mods = [jnp.swapaxes(mods_all[i], 0, 1) for i in range(depth)]

    s = jnp.concatenate([x, ctx], axis=1).reshape(n_batch * (seq + ctx_len), d)
    y = None
    c_now = c_tiles
    rw = functools.partial(_rowwise, n_batch=n_batch, x_tiles=x_tiles)

    for i in range(depth):
        kind, j = i % N_MIXERS, i // N_MIXERS
        ctx_in, ctx_out = i <= last_ret, i < last_ret
        assert ctx_in == (c_now > 0)
        g = norm_gain[i]
        if i == 0:
            (h,) = rw(s, None, None, (mods[i], g, 0), c_tiles_in=c_now, c_tiles_out=c_now, h_dtype=bf16)
        else:
            s, h = rw(s, y, (mods[i - 1], norm_gain[i - 1], 2, 0.5), (mods[i], g, 0),
                      c_tiles_in=c_now, c_tiles_out=c_now, h_dtype=bf16)
        y = _ffn(h, wg[i, 0], wu[i, 0], wd[i, 0])
        s, h = rw(s, y, (mods[i], g, 0, 0.5), (mods[i], g, 1),
                  c_tiles_in=c_now, c_tiles_out=c_now, h_dtype=bf16 if kind == 0 else f32)
        rows_b = (x_tiles + c_now) * ROW_TILE
        if kind == 0:
            assert c_now > 0
            p = _ret_inproj(h, _prep_ret_in(w_ret_in[j], d), cos, sin)
            o = _retention_core(p.reshape(n_batch, rows_b, 5 * d), ret_decay_logit[j],
                                n_batch=n_batch, n_heads=n_heads, n_x=seq // RET_CHUNK,
                                n_c=ctx_len // RET_CHUNK, ctx_out=ctx_out)
            y = _matmul(o.reshape(n_batch * rows_b, d), w_rout[j], f32)
        else:
            pm = _pool_means(h.reshape(n_batch, rows_b, d), n_x=seq, n_c=c_now * ROW_TILE)
            y = _pool_project(pm.reshape(n_batch * rows_b, d), w_pl[j], pool_scale[j])
        c_next = c_now if ctx_out else 0
        s, h = rw(s, y, (mods[i], g, 1, 1.0), (mods[i], g, 2),
                  c_tiles_in=c_now, c_tiles_out=c_next, h_dtype=bf16)
        c_now = c_next
        y = _ffn(h, wg[i, 1], wu[i, 1], wd[i, 1])

    assert c_now == 0
    (s,) = rw(s, y, (mods[depth - 1], norm_gain[depth - 1], 2, 0.5), None,
              c_tiles_in=0, c_tiles_out=0, h_dtype=bf16)
    return s.reshape(n_batch, seq, d)
```

```python
import functools

import numpy as np
import jax
import jax.numpy as jnp
from jax import lax
from jax.experimental import pallas as pl
from jax.experimental.pallas import tpu as pltpu

GRID_W = 64
RET_HEAD_DIM = 256
RET_CHUNK = 128
ROPE_THETA = 10000.0
POOL_WINDOWS = (2, 4, 8, 16)
N_MIXERS = 2
N_MOD = 9
EPS = 1e-6

LANES = 128
ROW_TILE = 256
MM_ROW_TILE = 1024
MM_COL_TILE = 512
RET_UNROLL = 4
POOL_SLAB = 256
POOL_UNROLL = 4
POOL_PAD_ROWS = 8
MIB = 1024 * 1024

f32 = jnp.float32
bf16 = jnp.bfloat16


def _pick(n, pref, align):
    t = min(pref, n)
    t -= t % align
    while t >= align:
        if n % t == 0:
            return t
        t -= align
    raise ValueError(f"no tile for {n} (pref {pref}, align {align})")


def _params(vmem_mib, n_grid):
    return pltpu.CompilerParams(
        dimension_semantics=("arbitrary",) * n_grid,
        vmem_limit_bytes=int(vmem_mib * MIB))


def _dot(a, b):
    return jnp.dot(a, b, preferred_element_type=f32)


def _silu(v):
    hv = 0.5 * v
    return hv + hv * jnp.tanh(hv)


def _w_spec(w, lead, k, tn):
    return pl.BlockSpec((None,) * len(lead) + (k, tn), lambda i, j: tuple(lead) + (0, j))


def _ada_kernel(c_ref, wd_ref, wu_ref, b_ref, o_ref, t_ref):
    @pl.when(pl.program_id(1) == 0)
    def _():
        t_ref[...] = _dot(_silu(c_ref[...]).astype(bf16), wd_ref[0].astype(bf16))

    o_ref[0, 0] = _dot(t_ref[...].astype(bf16), wu_ref[0].astype(bf16)) + b_ref[0, 0]


def _ada_modulation(cvecs, w_down, w_up, b):
    depth, d, r = w_down.shape
    m = cvecs.shape[0]
    return pl.pallas_call(
        _ada_kernel,
        grid=(depth, N_MOD),
        in_specs=[
            pl.BlockSpec((m, d), lambda l, j: (0, 0)),
            pl.BlockSpec((1, d, r), lambda l, j: (l, 0, 0)),
            pl.BlockSpec((1, r, d), lambda l, j: (l, 0, j)),
            pl.BlockSpec((1, 1, 1, d), lambda l, j: (l, j, 0, 0)),
        ],
        out_specs=pl.BlockSpec((1, 1, m, d), lambda l, j: (l, j, 0, 0)),
        out_shape=jax.ShapeDtypeStruct((depth, N_MOD, m, d), f32),
        scratch_shapes=[pltpu.VMEM((m, r), f32)],
        compiler_params=_params(48, 2),
        name="ada_modulation",
    )(cvecs, w_down, w_up, b.reshape(depth, N_MOD, 1, d))


def _pre_norm(x, mods_ref, gains_ref, k, inv_d):
    ms = jnp.sum(x * x, axis=-1, keepdims=True) * inv_d
    amp = gains_ref[2 * k:2 * k + 1, :] * (1.0 + mods_ref[0, 3 * k + 1:3 * k + 2, :])
    return (x * lax.rsqrt(ms + EPS)) * amp + mods_ref[0, 3 * k:3 * k + 1, :]


def _rowwise_kernel(*refs, post, pre, inv_d):
    it = iter(refs)
    x_ref = next(it)
    if post is not None:
        y_ref, mpost_ref, gpost_ref = next(it), next(it), next(it)
    if pre is not None:
        mpre_ref, gpre_ref = next(it), next(it)
    if post is not None:
        xo_ref = next(it)
    if pre is not None:
        h_ref = next(it)

    x = x_ref[...]
    if post is not None:
        k, weight = post
        y = y_ref[...]
        ms = jnp.sum(y * y, axis=-1, keepdims=True) * inv_d
        coef = (weight * mpost_ref[0, 3 * k + 2:3 * k + 3, :]) * gpost_ref[2 * k + 1:2 * k + 2, :]
        x = x + (y * lax.rsqrt(ms + EPS)) * coef
        xo_ref[...] = x
    if pre is not None:
        h_ref[...] = _pre_norm(x, mpre_ref, gpre_ref, pre, inv_d).astype(h_ref.dtype)


def _rowwise(x, y, post, pre, *, n_batch, x_tiles, c_tiles_in, c_tiles_out, h_dtype):
    d = x.shape[1]
    tm = ROW_TILE
    per_in = x_tiles + c_tiles_in
    per_out = x_tiles + c_tiles_out
    n_tiles = n_batch * per_out

    def in_tile(t):
        return (t // per_out) * per_in + t % per_out

    def mod_row(t):
        return jnp.where(t % per_out < x_tiles, t // per_out, n_batch)

    row_spec = pl.BlockSpec((tm, d), lambda t: (in_tile(t), 0))
    out_spec = pl.BlockSpec((tm, d), lambda t: (t, 0))
    mod_spec = pl.BlockSpec((1, N_MOD, d), lambda t: (mod_row(t), 0, 0))
    gain_spec = pl.BlockSpec((6, d), lambda t: (0, 0))

    args, in_specs, out_specs, out_shape = [x], [row_spec], [], []
    if post is not None:
        args += [y, post[0], post[1]]
        in_specs += [row_spec, mod_spec, gain_spec]
        out_specs.append(out_spec)
        out_shape.append(jax.ShapeDtypeStruct((n_tiles * tm, d), f32))
    if pre is not None:
        args += [pre[0], pre[1]]
        in_specs += [mod_spec, gain_spec]
        out_specs.append(out_spec)
        out_shape.append(jax.ShapeDtypeStruct((n_tiles * tm, d), h_dtype))

    kern = functools.partial(
        _rowwise_kernel,
        post=None if post is None else (post[2], post[3]),
        pre=None if pre is None else pre[2],
        inv_d=1.0 / d)
    return pl.pallas_call(
        kern,
        grid=(n_tiles,),
        in_specs=in_specs,
        out_specs=out_specs,
        out_shape=out_shape,
        compiler_params=_params(48, 1),
        name="rowwise",
    )(*args)


def _stream_kernel(x_ref, c_ref, m_ref, g_ref, s_ref, h_ref, *, k, x_tiles, per, inv_d):
    v = jnp.where(pl.program_id(0) % per < x_tiles, x_ref[...], c_ref[...])
    s_ref[...] = v
    h_ref[...] = _pre_norm(v, m_ref, g_ref, k, inv_d).astype(h_ref.dtype)


def _build_stream(x, ctx, mods, gains, k):
    n_batch, seq, d = x.shape
    tm = ROW_TILE
    x_tiles, c_tiles = seq // tm, ctx.shape[1] // tm
    per = x_tiles + c_tiles
    n_tiles = n_batch * per
    kern = functools.partial(_stream_kernel, k=k, x_tiles=x_tiles, per=per, inv_d=1.0 / d)
    out_spec = pl.BlockSpec((tm, d), lambda t: (t, 0))
    return pl.pallas_call(
        kern,
        grid=(n_tiles,),
        in_specs=[
            pl.BlockSpec((None, tm, d), lambda t: (t // per, jnp.minimum(t % per, x_tiles - 1), 0)),
            pl.BlockSpec((None, tm, d), lambda t: (t // per, jnp.maximum(t % per - x_tiles, 0), 0)),
            pl.BlockSpec((1, N_MOD, d), lambda t: (jnp.where(t % per < x_tiles, t // per, n_batch), 0, 0)),
            pl.BlockSpec((6, d), lambda t: (0, 0)),
        ],
        out_specs=[out_spec, out_spec],
        out_shape=[jax.ShapeDtypeStruct((n_tiles * tm, d), f32),
                   jax.ShapeDtypeStruct((n_tiles * tm, d), bf16)],
        compiler_params=_params(48, 1),
        name="build_stream",
    )(x, ctx, mods, gains)


def _mm_kernel(a_ref, w_ref, o_ref):
    o_ref[...] = _dot(a_ref[...], w_ref[...]).astype(o_ref.dtype)


def _matmul(a, w, lead, out_dtype):
    m, k = a.shape
    n = w.shape[-1]
    tm = _pick(m, MM_ROW_TILE, 8)
    tn = _pick(n, MM_COL_TILE, LANES)
    return pl.pallas_call(
        _mm_kernel,
        grid=(m // tm, n // tn),
        in_specs=[pl.BlockSpec((tm, k), lambda i, j: (i, 0)),
                  _w_spec(w, lead, k, tn)],
        out_specs=pl.BlockSpec((tm, tn), lambda i, j: (i, j)),
        out_shape=jax.ShapeDtypeStruct((m, n), out_dtype),
        compiler_params=_params(56, 2),
        name="matmul",
    )(a, w)


def _swiglu_kernel(h_ref, wg_ref, wu_ref, o_ref):
    h = h_ref[...]
    g = _dot(h, wg_ref[...])
    u = _dot(h, wu_ref[...])
    o_ref[...] = (_silu(g) * u).astype(o_ref.dtype)


def _swiglu_up(h, w_gate, w_up, lead):
    m, k = h.shape
    n = w_gate.shape[-1]
    tm = _pick(m, MM_ROW_TILE, 8)
    tn = _pick(n, MM_COL_TILE, LANES)
    return pl.pallas_call(
        _swiglu_kernel,
        grid=(m // tm, n // tn),
        in_specs=[pl.BlockSpec((tm, k), lambda i, j: (i, 0)),
                  _w_spec(w_gate, lead, k, tn),
                  _w_spec(w_up, lead, k, tn)],
        out_specs=pl.BlockSpec((tm, tn), lambda i, j: (i, j)),
        out_shape=jax.ShapeDtypeStruct((m, n), bf16),
        compiler_params=_params(56, 2),
        name="swiglu_up",
    )(h, w_gate, w_up)


def _ffn(h, w_gate, w_up, w_down, lead):
    return _matmul(_swiglu_up(h, w_gate, w_up, lead), w_down, lead, f32)


def _inproj_kernel(h_ref, w_ref, cos_ref, sin_ref, o_ref, *, d_tiles, scale):
    j = pl.program_id(1)
    acc = _dot(h_ref[...], w_ref[...])
    half = RET_HEAD_DIM // 2

    def rope(mult):
        cos, sin = cos_ref[...], sin_ref[...]
        for hh in range(acc.shape[1] // RET_HEAD_DIM):
            c0 = hh * RET_HEAD_DIM
            a = acc[:, c0:c0 + half]
            b = acc[:, c0 + half:c0 + RET_HEAD_DIM]
            o_ref[:, c0:c0 + half] = ((a * cos - b * sin) * mult).astype(o_ref.dtype)
            o_ref[:, c0 + half:c0 + RET_HEAD_DIM] = ((a * sin + b * cos) * mult).astype(o_ref.dtype)

    @pl.when(j < d_tiles)
    def _():
        rope(scale)

    @pl.when(jnp.logical_and(j >= d_tiles, j < 2 * d_tiles))
    def _():
        rope(1.0)

    @pl.when(jnp.logical_and(j >= 2 * d_tiles, j < 3 * d_tiles))
    def _():
        o_ref[...] = acc.astype(o_ref.dtype)

    @pl.when(j >= 3 * d_tiles)
    def _():
        o_ref[...] = _silu(acc).astype(o_ref.dtype)


def _ret_inproj(h, w_in, lead, cos, sin):
    m, d = h.shape
    n = w_in.shape[-1]
    tm = _pick(m, MM_ROW_TILE, 8)
    tn = _pick(d, MM_COL_TILE, RET_HEAD_DIM)
    kern = functools.partial(_inproj_kernel, d_tiles=d // tn, scale=RET_HEAD_DIM ** -0.5)
    tab_spec = pl.BlockSpec((tm, RET_HEAD_DIM // 2), lambda i, j: (i, 0))
    return pl.pallas_call(
        kern,
        grid=(m // tm, n // tn),
        in_specs=[pl.BlockSpec((tm, d), lambda i, j: (i, 0)),
                  _w_spec(w_in, lead, d, tn),
                  tab_spec, tab_spec],
        out_specs=pl.BlockSpec((tm, tn), lambda i, j: (i, j)),
        out_shape=jax.ShapeDtypeStruct((m, n), bf16),
        compiler_params=_params(56, 2),
        name="ret_inproj",
    )(h, w_in, cos, sin)


def _ret_kernel(lg_ref, q_ref, k_ref, v_ref, gf_ref, gb_ref, o_ref, s_ref, acc_ref, tab_ref,
                *, n_x, n_c, ctx_out):
    C, DH = RET_CHUNK, RET_HEAD_DIM
    ri = lax.broadcasted_iota(jnp.int32, (C, C), 0).astype(f32)
    ci = lax.broadcasted_iota(jnp.int32, (C, C), 1).astype(f32)
    rw = lax.broadcasted_iota(jnp.int32, (C, DH), 0).astype(f32)
    for d in range(2):
        z = lg_ref[d, 0]
        lg = jnp.minimum(z, 0.0) - jnp.log1p(jnp.exp(-jnp.abs(z)))
        rel = (ri - ci) if d == 0 else (ci - ri)
        tab_ref[d, 0, :, :C] = jnp.where(rel >= 0, jnp.exp(lg[:, :C] * jnp.maximum(rel, 0.0)), 0.0)
        tab_ref[d, 1] = jnp.exp(lg * (rw + 1.0)) if d == 0 else jnp.exp(lg * (C - rw))
        tab_ref[d, 2] = jnp.exp(lg * (C - 1.0 - rw)) if d == 0 else jnp.exp(lg * rw)
        tab_ref[d, 3] = jnp.exp(lg * (0.0 * rw + C))
    s_ref[...] = jnp.zeros_like(s_ref)
    gate_refs = (gf_ref, gb_ref)

    def one_dir(d, c, first, want_o):
        r0 = c * C
        if not isinstance(r0, int):
            r0 = pl.multiple_of(r0, C)
        rows = pl.ds(r0, C)
        q = q_ref[0, rows, :]
        k = k_ref[0, rows, :]
        v = v_ref[0, rows, :]
        s_old = s_ref[d]
        kd = (k.astype(f32) * tab_ref[d, 2]).astype(bf16)
        s_ref[d] = tab_ref[d, 3, 0:1, :] * s_old + lax.dot_general(
            kd, v, (((0,), (0,)), ((), ())), preferred_element_type=f32)
        if not want_o:
            return
        scores = lax.dot_general(q, k, (((1,), (1,)), ((), ())), preferred_element_type=f32)
        pm = (scores * tab_ref[d, 0, :, :C]).astype(bf16)
        o = _dot(pm, v) + tab_ref[d, 1] * _dot(q, s_old.astype(bf16))
        mu = jnp.mean(o, axis=-1, keepdims=True)
        dl = o - mu
        var = jnp.mean(dl * dl, axis=-1, keepdims=True)
        res = dl * lax.rsqrt(var + EPS) * gate_refs[d][0, rows, :].astype(f32)
        if first:
            acc_ref[rows, :] = res
        else:
            o_ref[0, rows, :] = (acc_ref[rows, :] + res).astype(o_ref.dtype)

    def phase(base, n, want_o):
        assert n % 2 == 0
        half = n // 2

        def pair(t, first):
            one_dir(0, base + t, first, want_o)
            one_dir(1, base + n - 1 - t, first, want_o)

        for first, lo in ((True, 0), (False, half)):
            if half <= RET_UNROLL:
                for t in range(lo, lo + half):
                    pair(t, first)
            else:
                assert half % RET_UNROLL == 0

                def body(u, carry, first=first, lo=lo):
                    for r in range(RET_UNROLL):
                        pair(lo + u * RET_UNROLL + r, first)
                    return carry
                lax.fori_loop(0, half // RET_UNROLL, body, 0)

    phase(n_x, n_c, ctx_out)
    if not ctx_out:
        o_ref[0, n_x * C:(n_x + n_c) * C, :] = jnp.zeros((n_c * C, DH), o_ref.dtype)
    phase(0, n_x, True)


def _retention_core(p, decay_logit, *, n_batch, n_heads, n_x, n_c, ctx_out):
    rows = (n_x + n_c) * RET_CHUNK
    dh = RET_HEAD_DIM
    lg = jnp.broadcast_to(decay_logit.astype(f32)[:, :, None, None], (2, n_heads, 1, dh))

    def col(g):
        return pl.BlockSpec((1, rows, dh), lambda b, h: (b, 0, g * n_heads + h))

    kern = functools.partial(_ret_kernel, n_x=n_x, n_c=n_c, ctx_out=ctx_out)
    return pl.pallas_call(
        kern,
        grid=(n_batch, n_heads),
        in_specs=[pl.BlockSpec((2, 1, 1, dh), lambda b, h: (0, h, 0, 0)),
                  col(0), col(1), col(2), col(3), col(4)],
        out_specs=pl.BlockSpec((1, rows, dh), lambda b, h: (b, 0, h)),
        out_shape=jax.ShapeDtypeStruct((n_batch, rows, n_heads * dh), bf16),
        scratch_shapes=[pltpu.VMEM((2, dh, dh), f32),
                        pltpu.VMEM((rows, dh), f32),
                        pltpu.VMEM((2, 4, RET_CHUNK, dh), f32)],
        compiler_params=_params(56, 2),
        name="retention_core",
    )(lg, p, p, p, p, p)


def _rope_tables(n_batch, seq, ctx_len):
    n_freq = RET_HEAD_DIM // 4
    inv = ROPE_THETA ** (-jnp.arange(n_freq, dtype=f32) / n_freq)
    pos = jnp.arange(seq)
    ang = jnp.concatenate([(pos // GRID_W)[:, None].astype(f32) * inv,
                           (pos % GRID_W)[:, None].astype(f32) * inv], axis=-1)
    cos = jnp.concatenate([jnp.cos(ang), jnp.ones((ctx_len, 2 * n_freq), f32)], axis=0)
    sin = jnp.concatenate([jnp.sin(ang), jnp.zeros((ctx_len, 2 * n_freq), f32)], axis=0)
    return jnp.tile(cos, (n_batch, 1)), jnp.tile(sin, (n_batch, 1))


def _prep_in_kernel(w_ref, perm_ref, o_ref, *, n_qk):
    wb = w_ref[...].astype(bf16)

    @pl.when(pl.program_id(1) < n_qk)
    def _():
        o_ref[...] = _dot(wb, perm_ref[...]).astype(bf16)

    @pl.when(pl.program_id(1) >= n_qk)
    def _():
        o_ref[...] = wb


def _prep_ret_in(w_in):
    n_layers, d, n = w_in.shape
    dh = RET_HEAD_DIM
    perm = np.zeros((dh, dh), np.float32)
    perm[2 * np.arange(dh // 2), np.arange(dh // 2)] = 1.0
    perm[2 * np.arange(dh // 2) + 1, dh // 2 + np.arange(dh // 2)] = 1.0
    kern = functools.partial(_prep_in_kernel, n_qk=2 * d // dh)
    blk = pl.BlockSpec((None, d, dh), lambda l, j: (l, 0, j))
    return pl.pallas_call(
        kern,
        grid=(n_layers, n // dh),
        in_specs=[blk, pl.BlockSpec((dh, dh), lambda l, j: (0, 0))],
        out_specs=blk,
        out_shape=jax.ShapeDtypeStruct((n_layers, d, n), bf16),
        compiler_params=_params(32, 2),
        name="prep_ret_in",
    )(w_in, jnp.asarray(perm, bf16))


def _pool_kernel(h_ref, band_ref, o_ref, pad_ref, inv_ref, *, n_x, n_c, grid_rows):
    g = pl.program_id(0)
    new_group = jnp.logical_and(pl.program_id(1) == 0, pl.program_id(2) == 0)
    top = POOL_PAD_ROWS * GRID_W
    ct = h_ref.shape[2]
    n_slabs = n_x // POOL_SLAB
    unroll = POOL_UNROLL if n_slabs % POOL_UNROLL == 0 else 1

    def band_sum(band, t):
        hi = t.astype(bf16)
        lo = (t - hi.astype(f32)).astype(bf16)
        return _dot(band, hi) + _dot(band, lo)

    def slab_loop(fn):
        def body(u, carry):
            for r in range(unroll):
                fn(pl.multiple_of((u * unroll + r) * POOL_SLAB, POOL_SLAB))
            return carry
        lax.fori_loop(0, n_slabs // unroll, body, 0)

    def body(w):
        half = w // 2

        @pl.when(new_group)
        def _():
            def counts(r0):
                tok = r0 + lax.broadcasted_iota(jnp.int32, (POOL_SLAB, LANES), 0)
                r = tok // GRID_W
                c = tok % GRID_W
                cr = jnp.minimum(r + half, grid_rows) - jnp.maximum(r - half, 0)
                cc = jnp.minimum(c + half, GRID_W) - jnp.maximum(c - half, 0)
                inv_ref[pl.ds(r0, POOL_SLAB), :] = 1.0 / (cr * cc).astype(f32)
            slab_loop(counts)
            if n_c:
                i = lax.broadcasted_iota(jnp.int32, (n_c, LANES), 0)
                cnt = jnp.minimum(i + half, n_c) - jnp.maximum(i - half, 0)
                inv_ref[n_x:n_x + n_c, :] = 1.0 / cnt.astype(f32)

        band2d = band_ref[0, 0]
        pad_ref[0:top, :] = jnp.zeros((top, ct), f32)
        pad_ref[top + n_x:2 * top + n_x, :] = jnp.zeros((top, ct), f32)

        def col_pass(r0):
            pad_ref[pl.ds(top + r0, POOL_SLAB), :] = band_sum(band2d, h_ref[0, pl.ds(r0, POOL_SLAB), :])

        slab_loop(col_pass)

        def row_pass(r0):
            acc = pad_ref[pl.ds(top + r0 - half * GRID_W, POOL_SLAB), :]
            for dd in range(-half + 1, half):
                acc = acc + pad_ref[pl.ds(top + r0 + dd * GRID_W, POOL_SLAB), :]
            rows = pl.ds(r0, POOL_SLAB)
            inv = jnp.tile(inv_ref[rows, :], (1, ct // LANES))
            o_ref[0, rows, :] = (acc * inv - h_ref[0, rows, :]).astype(o_ref.dtype)

        slab_loop(row_pass)

        if n_c:
            t = h_ref[0, n_x:n_x + n_c, :]
            inv = jnp.tile(inv_ref[n_x:n_x + n_c, :], (1, ct // LANES))
            o_ref[0, n_x:n_x + n_c, :] = (band_sum(band_ref[0, 1], t) * inv - t).astype(o_ref.dtype)

    for gi, w in enumerate(POOL_WINDOWS):
        pl.when(g == gi)(functools.partial(body, w))


def _pool_bands(n_c):
    n1 = n_c if n_c else POOL_SLAB
    assert n1 == POOL_SLAB
    i = np.arange(POOL_SLAB)
    bands = np.zeros((len(POOL_WINDOWS), 2, POOL_SLAB, POOL_SLAB), np.float32)
    for gi, w in enumerate(POOL_WINDOWS):
        half = w // 2
        c = i % GRID_W
        lo = (i - c) + np.clip(c - half, 0, GRID_W)
        hi = (i - c) + np.clip(c + half, 0, GRID_W)
        bands[gi, 0] = (i[None, :] >= lo[:, None]) & (i[None, :] < hi[:, None])
        lo1 = np.clip(i - half, 0, n1)
        hi1 = np.clip(i + half, 0, n1)
        bands[gi, 1] = (i[None, :] >= lo1[:, None]) & (i[None, :] < hi1[:, None])
    return jnp.asarray(bands, bf16)


def _pool_means(h, *, n_x, n_c):
    n_batch, rows, d = h.shape
    n_groups = len(POOL_WINDOWS)
    gd = d // n_groups
    ct = _pick(gd, 256, LANES)
    tpg = gd // ct
    grid_rows = n_x // GRID_W
    assert n_x % POOL_SLAB == 0 and POOL_SLAB % GRID_W == 0
    kern = functools.partial(_pool_kernel, n_x=n_x, n_c=n_c, grid_rows=grid_rows)
    blk = pl.BlockSpec((1, rows, ct), lambda g, b, j: (b, 0, g * tpg + j))
    return pl.pallas_call(
        kern,
        grid=(n_groups, n_batch, tpg),
        in_specs=[blk, pl.BlockSpec((1, 2, POOL_SLAB, POOL_SLAB), lambda g, b, j: (g, 0, 0, 0))],
        out_specs=blk,
        out_shape=jax.ShapeDtypeStruct((n_batch, rows, d), bf16),
        scratch_shapes=[pltpu.VMEM((n_x + 2 * POOL_PAD_ROWS * GRID_W, ct), f32),
                        pltpu.VMEM((rows, LANES), f32)],
        compiler_params=_params(48, 3),
        name="pool_means",
    )(h, _pool_bands(n_c))


def _pool_mm_kernel(p_ref, w_ref, s_ref, o_ref):
    o_ref[...] = _dot(p_ref[...], w_ref[...]) * s_ref[...]


def _pool_project(p, w_pool, lead, scale):
    m, d = p.shape
    n_groups, gd = w_pool.shape[-3], w_pool.shape[-1]
    tm = _pick(m, MM_ROW_TILE, 8)
    return pl.pallas_call(
        _pool_mm_kernel,
        grid=(m // tm, n_groups),
        in_specs=[pl.BlockSpec((tm, gd), lambda i, g: (i, g)),
                  pl.BlockSpec((None,) * (len(lead) + 1) + (gd, gd), lambda i, g: tuple(lead) + (g, 0, 0)),
                  pl.BlockSpec((1, gd), lambda i, g: (0, g))],
        out_specs=pl.BlockSpec((tm, gd), lambda i, g: (i, g)),
        out_shape=jax.ShapeDtypeStruct((m, d), f32),
        compiler_params=_params(48, 2),
        name="pool_project",
    )(p, w_pool, scale.reshape(1, d))


def kernel(x, c, ctx, c_ctx, w_ada_down, w_ada_up, b_ada, norm_gain, w_ffn_gate, w_ffn_up,
           w_ffn_down, w_ret_in, w_ret_out, ret_decay_logit, w_pool, pool_scale):
    n_batch, seq, d = x.shape
    ctx_len = ctx.shape[1]
    depth = w_ada_down.shape[0]
    n_heads = d // RET_HEAD_DIM
    assert seq % ROW_TILE == 0 and ctx_len % ROW_TILE == 0 and n_batch < 8
    x_tiles, c_tiles = seq // ROW_TILE, ctx_len // ROW_TILE
    last_ret = ((depth - 1) // N_MIXERS) * N_MIXERS

    wg, wu, wd = w_ffn_gate.astype(bf16), w_ffn_up.astype(bf16), w_ffn_down.astype(bf16)
    w_rin, w_rout, w_pl = _prep_ret_in(w_ret_in), w_ret_out.astype(bf16), w_pool.astype(bf16)
    cos, sin = _rope_tables(n_batch, seq, ctx_len)

    cvecs = jnp.zeros((8, d), f32).at[:n_batch].set(c).at[n_batch].set(c_ctx)
    mods_all = _ada_modulation(cvecs, w_ada_down, w_ada_up, b_ada)
    mods = [jnp.swapaxes(mods_all[i], 0, 1) for i in range(depth)]

    c_now = c_tiles
    rw = functools.partial(_rowwise, n_batch=n_batch, x_tiles=x_tiles)

    for i in range(depth):
        kind, j = i % N_MIXERS, i // N_MIXERS
        ctx_in, ctx_out = i <= last_ret, i < last_ret
        assert ctx_in == (c_now > 0)
        g = norm_gain[i]
        if i == 0:
            s, h = _build_stream(x, ctx, mods[i], g, 0)
        else:
            s, h = rw(s, y, (mods[i - 1], norm_gain[i - 1], 2, 0.5), (mods[i], g, 0),
                      c_tiles_in=c_now, c_tiles_out=c_now, h_dtype=bf16)
        y = _ffn(h, wg, wu, wd, (i, 0))
        s, h = rw(s, y, (mods[i], g, 0, 0.5), (mods[i], g, 1),
                  c_tiles_in=c_now, c_tiles_out=c_now, h_dtype=bf16 if kind == 0 else f32)
        rows_b = (x_tiles + c_now) * ROW_TILE
        if kind == 0:
            assert c_now > 0
            p = _ret_inproj(h, w_rin, (j,), cos, sin)
            o = _retention_core(p.reshape(n_batch, rows_b, 5 * d), ret_decay_logit[j],
                                n_batch=n_batch, n_heads=n_heads, n_x=seq // RET_CHUNK,
                                n_c=ctx_len // RET_CHUNK, ctx_out=ctx_out)
            y = _matmul(o.reshape(n_batch * rows_b, d), w_rout, (j,), f32)
        else:
            pm = _pool_means(h.reshape(n_batch, rows_b, d), n_x=seq, n_c=c_now * ROW_TILE)
            y = _pool_project(pm.reshape(n_batch * rows_b, d), w_pl, (j,), pool_scale[j])
        c_next = c_now if ctx_out else 0
        s, h = rw(s, y, (mods[i], g, 1, 1.0), (mods[i], g, 2),
                  c_tiles_in=c_now, c_tiles_out=c_next, h_dtype=bf16)
        c_now = c_next
        y = _ffn(h, wg, wu, wd, (i, 1))

    assert c_now == 0
    (s,) = rw(s, y, (mods[depth - 1], norm_gain[depth - 1], 2, 0.5), None,
              c_tiles_in=0, c_tiles_out=0, h_dtype=bf16)
    return s.reshape(n_batch, seq, d)
```
